```python
import jax, jax.numpy as jnp
from jax import lax
import numpy as np

D_MODEL = 1024
BATCH = 32
SEQ = 256
DEPTH = 2
DEC_BATCH = 4
DEC_SEQ = 4096
PAST_LEN = 256

GRID_W = 64
N_HEADS_ATTN = 8
N_KV_ATTN = 2
HEAD_DIM = 64
GQA_GROUP = N_HEADS_ATTN // N_KV_ATTN
WINDOW = 128
ATTN_BLOCK = 128
CTX_Q_BLOCK = 128
ROPE_BASE = 10000.0
ATTN_SCALE = HEAD_DIM ** -0.5
NEG_INF = -1e30
N_HEADS_GLA = 4
GLA_DK = 64
GLA_DV = 128
GLA_RANK = 16
GLA_TAU = 16.0
GLA_CHUNK = 64
N_EXPERTS = 64
TOP_K = 8
D_EXPERT = 256
D_SHARED = 256
ROUTE_SCALE = 2.5
ALPHA_RES = (2.0 * DEPTH) ** 0.25
BETA_INIT = (8.0 * DEPTH) ** -0.25
LN_EPS = 1e-5
Q_W = N_HEADS_ATTN * HEAD_DIM
KV_W = N_KV_ATTN * HEAD_DIM
GK_W = N_HEADS_GLA * GLA_DK
GV_W = N_HEADS_GLA * GLA_DV
IN_WIDTHS = (Q_W, KV_W, KV_W, GK_W, GK_W, GV_W, GV_W, GLA_RANK, GLA_RANK, D_MODEL, D_MODEL)
IN_W = sum(IN_WIDTHS)

kernel_name = 'hybrid_swa_gla_moe_diffusion_step'


def layer_norm(x, g, b):
    xf = x.astype(jnp.float32)
    mu = jnp.mean(xf, axis=-1, keepdims=True)
    var = jnp.mean(jnp.square(xf - mu), axis=-1, keepdims=True)
    return ((xf - mu) * lax.rsqrt(var + LN_EPS)).astype(x.dtype) * g + b


def head_rms_norm(x, g):
    xf = x.astype(jnp.float32)
    y = xf * lax.rsqrt(jnp.mean(jnp.square(xf), axis=-1, keepdims=True) + LN_EPS)
    return y.astype(x.dtype) * g


def axis_rope(x, pos):
    n = x.shape[-1] // 2
    inv = ROPE_BASE ** (-jnp.arange(n, dtype=jnp.float32) / n)
    ang = pos.astype(jnp.float32)[:, None] * inv[None, :]
    cos = jnp.cos(ang)[None, :, None, :]
    sin = jnp.sin(ang)[None, :, None, :]
    x1 = x[..., :n].astype(jnp.float32)
    x2 = x[..., n:].astype(jnp.float32)
    return jnp.concatenate([x1 * cos - x2 * sin, x1 * sin + x2 * cos], axis=-1).astype(x.dtype)


def rope_2d(x):
    L = x.shape[1]
    rows = L // GRID_W
    row = jnp.repeat(jnp.arange(rows), GRID_W)
    col = jnp.tile(jnp.arange(GRID_W), rows)
    half = x.shape[-1] // 2
    return jnp.concatenate([axis_rope(x[..., :half], row), axis_rope(x[..., half:], col)], axis=-1)


def context_attention(q, k, v, sink):
    B, L = q.shape[:2]
    nqb = L // CTX_Q_BLOCK
    qb = jnp.moveaxis(q.reshape(B, nqb, CTX_Q_BLOCK, N_KV_ATTN, GQA_GROUP, HEAD_DIM), 1, 0)
    sink_l = jnp.broadcast_to(sink.reshape(N_KV_ATTN, GQA_GROUP)[None, :, :, None, None].astype(jnp.float32),
                              (B, N_KV_ATTN, GQA_GROUP, CTX_Q_BLOCK, 1))

    def block(qblk):
        s = jnp.einsum('bqkgd,bckd->bkgqc', qblk, k).astype(jnp.float32) * ATTN_SCALE
        p = jax.nn.softmax(jnp.concatenate([s, sink_l], axis=-1), axis=-1)
        return jnp.einsum('bkgqc,bckd->bqkgd', p[..., :L].astype(v.dtype), v)

    o = lax.map(block, qb)
    return jnp.moveaxis(o, 0, 1).reshape(B, L, Q_W)


def latent_attention(q, k, v, ck, cv, sink):
    B, L = q.shape[:2]
    W = ATTN_BLOCK
    nb = L // W
    Lc = ck.shape[2]
    qb = q.reshape(B, nb, W, N_KV_ATTN, GQA_GROUP, HEAD_DIM)

    def band(t):
        tp = jnp.pad(t, ((0, 0), (W, W), (0, 0), (0, 0))).reshape(B, nb + 2, W, N_KV_ATTN, HEAD_DIM)
        return jnp.concatenate([tp[:, :-2], tp[:, 1:-1], tp[:, 2:]], axis=2)

    kb, vb = band(k), band(v)
    s_loc = jnp.einsum('bnqkgd,bnjkd->bkgnqj', qb, kb).astype(jnp.float32) * ATTN_SCALE
    a = jnp.arange(W)[:, None]
    j = jnp.arange(3 * W)[None, :]
    kpos = jnp.arange(nb)[:, None, None] * W - W + j[None]
    valid = (jnp.abs(j - W - a) <= WINDOW)[None] & (kpos >= 0) & (kpos < L)
    s_loc = jnp.where(valid, s_loc, NEG_INF)
    s_ctx = jnp.einsum('bnqkgd,bkcd->bkgnqc', qb, ck).astype(jnp.float32) * ATTN_SCALE
    sink_l = jnp.broadcast_to(sink.reshape(N_KV_ATTN, GQA_GROUP)[None, :, :, None, None, None].astype(jnp.float32),
                              (B, N_KV_ATTN, GQA_GROUP, nb, W, 1))
    p = jax.nn.softmax(jnp.concatenate([s_loc, s_ctx, sink_l], axis=-1), axis=-1)
    p_loc = p[..., :3 * W].astype(v.dtype)
    p_ctx = p[..., 3 * W:3 * W + Lc].astype(cv.dtype)
    o = (jnp.einsum('bkgnqj,bnjkd->bnqkgd', p_loc, vb)
         + jnp.einsum('bkgnqc,bkcd->bnqkgd', p_ctx, cv))
    return o.reshape(B, L, Q_W)


def gla_chunked(q, k, v, log_a, s0):
    B, H, L, dk = q.shape
    dv = v.shape[-1]
    n = L // GLA_CHUNK
    f32 = jnp.float32
    qc = q.astype(f32).reshape(B, H, n, GLA_CHUNK, dk)
    kc = k.astype(f32).reshape(B, H, n, GLA_CHUNK, dk)
    vc = v.astype(f32).reshape(B, H, n, GLA_CHUNK, dv)
    b = jnp.cumsum(log_a.astype(f32).reshape(B, H, n, GLA_CHUNK, dk), axis=3)
    b_last = b[..., -1:, :]
    q_e = qc * jnp.exp(b)
    k_e = kc * jnp.exp(-b)
    k_end = kc * jnp.exp(b_last - b)
    lower = jnp.tril(jnp.ones((GLA_CHUNK, GLA_CHUNK), dtype=bool))
    a_intra = jnp.where(lower, jnp.einsum('bhnid,bhnjd->bhnij', q_e, k_e), 0.0)
    o_intra = jnp.einsum('bhnij,bhnje->bhnie', a_intra, vc)
    kv_chunk = jnp.einsum('bhncd,bhnce->bhnde', k_end, vc)
    decay = jnp.exp(b_last[..., 0, :])

    def step(s, inp):
        dec, kv = inp
        return dec[..., None] * s + kv, s

    s_fin, s_prev = lax.scan(step, s0.astype(f32), (jnp.moveaxis(decay, 2, 0), jnp.moveaxis(kv_chunk, 2, 0)))
    s_prev = jnp.moveaxis(s_prev, 0, 2)
    o_inter = jnp.einsum('bhncd,bhnde->bhnce', q_e, s_prev)
    o = (o_intra + o_inter).reshape(B, H, L, dv)
    return o.astype(v.dtype), s_fin.astype(v.dtype)


def gla_bidirectional(q, k, v, la_f, la_b, s0_f, s0_b):
    o_f, s_f = gla_chunked(q, k, v, la_f, s0_f)
    flip = lambda t: jnp.flip(t, axis=2)
    o_b, s_b = gla_chunked(flip(q), flip(k), flip(v), flip(la_b), s0_b)
    return o_f + flip(o_b), s_f, s_b


def token_mixer(h, ctx, w_in, b_gate, w_a2_f, b_a_f, w_a2_b, b_a_b, sink, gla_g, w_ba, w_bg, w_o):
    B, L, _ = h.shape
    splits = [int(s) for s in np.cumsum(IN_WIDTHS)[:-1]]
    q, k, v, gq, gk, gv, gr, af, ab, g_attn, g_gla = jnp.split(h @ w_in, splits, axis=-1)
    q = q.reshape(B, L, N_KV_ATTN, GQA_GROUP, HEAD_DIM)
    k = k.reshape(B, L, N_KV_ATTN, HEAD_DIM)
    v = v.reshape(B, L, N_KV_ATTN, HEAD_DIM)

    def gla_heads(t, d):
        return t.reshape(B, L, N_HEADS_GLA, d).transpose(0, 2, 1, 3)

    gq = gla_heads(gq, GLA_DK) * (GLA_DK ** -0.5)
    gk = gla_heads(gk, GLA_DK)
    gv = gla_heads(gv, GLA_DV)
    la_f = gla_heads(jax.nn.log_sigmoid((af @ w_a2_f + b_a_f).astype(jnp.float32)) / GLA_TAU, GLA_DK)
    la_b = gla_heads(jax.nn.log_sigmoid((ab @ w_a2_b + b_a_b).astype(jnp.float32)) / GLA_TAU, GLA_DK)

    if ctx is None:
        o_attn = context_attention(q, k, v, sink)
        s0 = jnp.zeros((B, N_HEADS_GLA, GLA_DK, GLA_DV), jnp.float32)
        o_gla, s_f, s_b = gla_bidirectional(gq, gk, gv, la_f, la_b, s0, s0)
        state = (k.transpose(0, 2, 1, 3), v.transpose(0, 2, 1, 3), jnp.stack([s_f, s_b], axis=1))
    else:
        ck, cv, s0_f, s0_b = ctx
        q = rope_2d(q.reshape(B, L, N_HEADS_ATTN, HEAD_DIM)).reshape(B, L, N_KV_ATTN, GQA_GROUP, HEAD_DIM)
        k = rope_2d(k)
        o_attn = latent_attention(q, k, v, ck, cv, sink)
        o_gla, _, _ = gla_bidirectional(gq, gk, gv, la_f, la_b, s0_f, s0_b)
        state = None

    o_gla = head_rms_norm(o_gla.transpose(0, 2, 1, 3), gla_g).reshape(B, L, GV_W) * jax.nn.silu(gr)
    merged = (jax.nn.sigmoid(g_attn + b_gate[0]) * (o_attn @ w_ba)
              + jax.nn.sigmoid(g_gla + b_gate[1]) * (o_gla @ w_bg))
    return merged @ w_o, state


def moe_ffn(h, w_router, router_bias, w1, w3, w2, ws1, ws3, ws2):
    B, L, D = h.shape
    t = h.reshape(B * L, D)
    s = jax.nn.sigmoid((t @ w_router).astype(jnp.float32))
    _, idx = lax.top_k(s + router_bias.astype(jnp.float32), TOP_K)
    sel = jnp.take_along_axis(s, idx, axis=-1)
    wts = sel / jnp.sum(sel, axis=-1, keepdims=True) * ROUTE_SCALE
    gates = jnp.sum(jax.nn.one_hot(idx, N_EXPERTS, dtype=jnp.float32) * wts[..., None], axis=1).astype(t.dtype)

    def expert(acc, ew):
        e1, e3, e2, g = ew
        return acc + ((jax.nn.silu(t @ e1) * (t @ e3)) @ e2) * g[:, None], None

    routed, _ = lax.scan(expert, jnp.zeros_like(t), (w1, w3, w2, gates.T))
    shared = (jax.nn.silu(t @ ws1) * (t @ ws3)) @ ws2
    return (routed + shared).reshape(B, L, D)


def trunk_layer(x, cond, ctx, w_mod, b_mod, w_in, b_gate, w_a2_f, b_a_f, w_a2_b, b_a_b, sink, gla_g,
                w_ba, w_bg, w_o, ln1_g, ln1_b, w_router, router_bias, w1, w3, w2, ws1, ws3, ws2, ln2_g, ln2_b):
    m = jax.nn.silu(cond) @ w_mod + b_mod
    shift1, scale1, gate1, shift2, scale2, gate2 = [t[..., None, :] for t in jnp.split(m, 6, axis=-1)]
    h = x * (1.0 + scale1) + shift1
    y, state = token_mixer(h, ctx, w_in, b_gate, w_a2_f, b_a_f, w_a2_b, b_a_b, sink, gla_g, w_ba, w_bg, w_o)
    x = layer_norm(ALPHA_RES * x + gate1 * y, ln1_g, ln1_b)
    h = x * (1.0 + scale2) + shift2
    y = moe_ffn(h, w_router, router_bias, w1, w3, w2, ws1, ws3, ws2)
    x = layer_norm(ALPHA_RES * x + gate2 * y, ln2_g, ln2_b)
    return x, state


def setup_inputs(seed: int = 0) -> dict:
    key = jax.random.key(seed)
    D = D_MODEL
    specs = [
        ('x_prompt', (BATCH, SEQ, D), 1.0, 0.0),
        ('x_sample', (DEC_BATCH, DEC_SEQ, D), 1.0, 0.0),
        ('cache_k', (DEC_BATCH, DEPTH, N_KV_ATTN, PAST_LEN, HEAD_DIM), 1.0, 0.0),
        ('cache_v', (DEC_BATCH, DEPTH, N_KV_ATTN, PAST_LEN, HEAD_DIM), 1.0, 0.0),
        ('state_gla', (DEC_BATCH, DEPTH, 2, N_HEADS_GLA, GLA_DK, GLA_DV), 2.0, 0.0),
        ('c', (DEC_BATCH, D), 1.0, 0.0),
        ('c_ctx', (D,), 1.0, 0.0),
        ('w_mod', (DEPTH, D, 6 * D), 0.5 * D ** -0.5, 0.0),
        ('b_mod', (DEPTH, 6 * D), 0.05, 0.0),
        ('w_in', (DEPTH, D, IN_W), D ** -0.5, 0.0),
        ('b_gate', (DEPTH, 2, D), 0.1, 0.0),
        ('w_a2_f', (DEPTH, GLA_RANK, GK_W), GLA_RANK ** -0.5, 0.0),
        ('b_a_f', (DEPTH, GK_W), 0.5, 1.0),
        ('w_a2_b', (DEPTH, GLA_RANK, GK_W), GLA_RANK ** -0.5, 0.0),
        ('b_a_b', (DEPTH, GK_W), 0.5, 1.0),
        ('sink', (DEPTH, N_HEADS_ATTN), 1.0, 0.0),
        ('gla_g', (DEPTH, N_HEADS_GLA, GLA_DV), 0.05, 1.0),
        ('w_ba', (DEPTH, Q_W, D), Q_W ** -0.5, 0.0),
        ('w_bg', (DEPTH, GV_W, D), GV_W ** -0.5, 0.0),
        ('w_o', (DEPTH, D, D), BETA_INIT * D ** -0.5, 0.0),
        ('ln1_g', (DEPTH, D), 0.05, 1.0),
        ('ln1_b', (DEPTH, D), 0.02, 0.0),
        ('w_router', (DEPTH, D, N_EXPERTS), D ** -0.5, 0.0),
        ('router_bias', (DEPTH, N_EXPERTS), 0.01, 0.0),
        ('w1', (DEPTH, N_EXPERTS, D, D_EXPERT), D ** -0.5, 0.0),
        ('w3', (DEPTH, N_EXPERTS, D, D_EXPERT), D ** -0.5, 0.0),
        ('w2', (DEPTH, N_EXPERTS, D_EXPERT, D), BETA_INIT * D_EXPERT ** -0.5, 0.0),
        ('ws1', (DEPTH, D, D_SHARED), D ** -0.5, 0.0),
        ('ws3', (DEPTH, D, D_SHARED), D ** -0.5, 0.0),
        ('ws2', (DEPTH, D_SHARED, D), BETA_INIT * D_SHARED ** -0.5, 0.0),
        ('ln2_g', (DEPTH, D), 0.05, 1.0),
        ('ln2_b', (DEPTH, D), 0.02, 0.0),
    ]
    keys = jax.random.split(key, len(specs))
    return {name: off + scale * jax.random.normal(k, shape, jnp.float32)
            for (name, shape, scale, off), k in zip(specs, keys)}


def reference(x_prompt, x_sample, cache_k, cache_v, state_gla, c, c_ctx, w_mod, b_mod, w_in, b_gate,
              w_a2_f, b_a_f, w_a2_b, b_a_b, sink, gla_g, w_ba, w_bg, w_o, ln1_g, ln1_b, w_router, router_bias,
              w1, w3, w2, ws1, ws3, ws2, ln2_g, ln2_b):
    xp = x_prompt
    xs = x_sample
    new_k, new_v, new_s = [], [], []
    for l in range(DEPTH):
        lw = (w_mod[l], b_mod[l], w_in[l], b_gate[l], w_a2_f[l], b_a_f[l], w_a2_b[l], b_a_b[l], sink[l],
              gla_g[l], w_ba[l], w_bg[l], w_o[l], ln1_g[l], ln1_b[l], w_router[l], router_bias[l],
              w1[l], w3[l], w2[l], ws1[l], ws3[l], ws2[l], ln2_g[l], ln2_b[l])
        xp, (k_l, v_l, s_l) = trunk_layer(xp, c_ctx, None, *lw)
        new_k.append(k_l)
        new_v.append(v_l)
        new_s.append(s_l)
        xs, _ = trunk_layer(xs, c, (cache_k[:, l], cache_v[:, l], state_gla[:, l, 0], state_gla[:, l, 1]), *lw)
    return (xp, xs, jnp.stack(new_k, axis=1), jnp.stack(new_v, axis=1), jnp.stack(new_s, axis=1))
```

```python
import functools

import jax
import jax.numpy as jnp
import numpy as np
from jax import lax
from jax.experimental import pallas as pl
from jax.experimental.pallas import tpu as pltpu

F32 = jnp.float32
BF16 = jnp.bfloat16

N_HEADS_ATTN = 8
N_KV_ATTN = 2
HEAD_DIM = 64
GQA_GROUP = N_HEADS_ATTN // N_KV_ATTN
WINDOW = 128
GRID_W = 64
ROPE_BASE = 10000.0
ATTN_SCALE = HEAD_DIM ** -0.5
NEG_INF = -1e30
N_HEADS_GLA = 4
GLA_DK = 64
GLA_DV = 128
GLA_RANK = 16
GLA_TAU = 16.0
GLA_CHUNK = 64
TOP_K = 8
ROUTE_SCALE = 2.5
LN_EPS = 1e-5
Q_W = N_HEADS_ATTN * HEAD_DIM
KV_W = N_KV_ATTN * HEAD_DIM
GK_W = N_HEADS_GLA * GLA_DK
GV_W = N_HEADS_GLA * GLA_DV

LANES = 128
MOD_ROWS = 8
VMEM_LIMIT = 56 * 1024 * 1024


def _cparams(sem):
    return pltpu.CompilerParams(dimension_semantics=sem, vmem_limit_bytes=VMEM_LIMIT)


def _sigmoid(x):
    return 1.0 / (1.0 + jnp.exp(-x))


def _silu(x):
    return x * _sigmoid(x)


def _log_sigmoid(x):
    return jnp.minimum(x, 0.0) - jnp.log(1.0 + jnp.exp(-jnp.abs(x)))


def _lane_iota(shape):
    return lax.broadcasted_iota(jnp.int32, shape, len(shape) - 1)


def _row_iota(shape):
    return lax.broadcasted_iota(jnp.int32, shape, 0)


def _dot(a, b):
    return jnp.dot(a, b, preferred_element_type=F32)


def _dot_nt(a, b):
    return lax.dot_general(a, b, (((1,), (1,)), ((), ())), preferred_element_type=F32)


def _dot_tn(a, b):
    return lax.dot_general(a, b, (((0,), (0,)), ((), ())), preferred_element_type=F32)


def _layer_norm(x, g, b):
    mu = jnp.mean(x, axis=-1, keepdims=True)
    xc = x - mu
    var = jnp.mean(xc * xc, axis=-1, keepdims=True)
    return xc * lax.rsqrt(var + LN_EPS) * g + b


def _mod_kernel(c_ref, w_ref, b_ref, o_ref):
    o_ref[0] = _dot(_silu(c_ref[...]), w_ref[0]) + b_ref[0]


def _modulation(cond, w_mod, b_mod):
    depth, d, n = w_mod.shape
    tn = n // 4
    return pl.pallas_call(
        _mod_kernel,
        grid=(depth, n // tn),
        in_specs=[
            pl.BlockSpec((MOD_ROWS, d), lambda l, j: (0, 0)),
            pl.BlockSpec((1, d, tn), lambda l, j: (l, 0, j)),
            pl.BlockSpec((1, 1, tn), lambda l, j: (l, 0, j)),
        ],
        out_specs=pl.BlockSpec((1, MOD_ROWS, tn), lambda l, j: (l, 0, j)),
        out_shape=jax.ShapeDtypeStruct((depth, MOD_ROWS, n), F32),
        compiler_params=_cparams(("arbitrary", "arbitrary")),
        name="modulation",
    )(cond, w_mod, b_mod.reshape(depth, 1, n))


def _swap16(x):
    lane = _lane_iota(x.shape)
    up = pltpu.roll(x, LANES - 16, 1)
    down = pltpu.roll(x, 16, 1)
    return jnp.where((lane & 31) < 16, up, down)


def _rope(x, cos, sin):
    out = []
    for j in range(x.shape[1] // LANES):
        xj = x[:, j * LANES:(j + 1) * LANES]
        out.append(xj * cos + _swap16(xj) * sin)
    return out[0] if len(out) == 1 else jnp.concatenate(out, axis=1)


def _in_proj_kernel(x_ref, mod_ref, wm_ref, wab_ref, wg_ref, wa2_ref, ba_ref, bg_ref, cos_ref, sin_ref,
                    q_ref, k_ref, v_ref, gq_ref, gk_ref, gv_ref, gr_ref, la_ref, sg_ref):
    m = mod_ref[0]
    h = (x_ref[...] * (1.0 + m[1:2, :]) + m[0:1, :]).astype(BF16)
    cos = cos_ref[...]
    sin = sin_ref[...]
    o = 0
    q = _dot(h, wm_ref[:, o:o + Q_W]); o += Q_W
    q_ref[...] = (_rope(q, cos, sin) * ATTN_SCALE).astype(BF16)
    k = _dot(h, wm_ref[:, o:o + KV_W]); o += KV_W
    k_ref[...] = _rope(k, cos, sin)
    v_ref[...] = _dot(h, wm_ref[:, o:o + KV_W]); o += KV_W
    gq_ref[...] = (_dot(h, wm_ref[:, o:o + GK_W]) * (GLA_DK ** -0.5)).astype(BF16); o += GK_W
    gk_ref[...] = _dot(h, wm_ref[:, o:o + GK_W]).astype(BF16); o += GK_W
    gv_ref[...] = _dot(h, wm_ref[:, o:o + GV_W]).astype(BF16); o += GV_W
    gr_ref[...] = _dot(h, wm_ref[:, o:o + GV_W]).astype(BF16)
    ab = _dot(h, wab_ref[...])
    z = jnp.dot(ab, wa2_ref[...], preferred_element_type=F32, precision=lax.Precision.HIGHEST) + ba_ref[...]
    la_ref[...] = _log_sigmoid(z) * (1.0 / GLA_TAU)
    sg_ref[...] = _sigmoid(_dot(h, wg_ref[...]) + bg_ref[...]).astype(BF16)


def _in_proj(x, mod_l, wm, wab, wg, wa2, ba, bg, cos_t, sin_t, *, tm, n_prompt, dec_seq):
    t, d = x.shape
    npt = n_prompt // tm
    spt = dec_seq // tm

    def mod_row(i):
        return jnp.where(i < npt, 0, 1 + (i - npt) // spt)

    def rope_blk(i):
        return jnp.where(i < npt, 0, 1 + (i - npt) % spt)

    row = lambda w: pl.BlockSpec((tm, w), lambda i: (i, 0))
    full = lambda a: pl.BlockSpec(a.shape, lambda i: (0,) * a.ndim)
    out_w = [(Q_W, BF16), (KV_W, F32), (KV_W, F32), (GK_W, BF16), (GK_W, BF16), (GV_W, BF16), (GV_W, BF16),
             (2 * GK_W, F32), (2 * d, BF16)]
    return pl.pallas_call(
        _in_proj_kernel,
        grid=(t // tm,),
        in_specs=[
            row(d),
            pl.BlockSpec((1, 6, d), lambda i: (mod_row(i), 0, 0)),
            full(wm), full(wab), full(wg), full(wa2), full(ba), full(bg),
            pl.BlockSpec((tm, LANES), lambda i: (rope_blk(i), 0)),
            pl.BlockSpec((tm, LANES), lambda i: (rope_blk(i), 0)),
        ],
        out_specs=[row(w) for w, _ in out_w],
        out_shape=[jax.ShapeDtypeStruct((t, w), dt) for w, dt in out_w],
        compiler_params=_cparams(("arbitrary",)),
        name="in_proj",
    )(x, mod_l, wm, wab, wg, wa2, ba, bg, cos_t, sin_t)


def _rope_tables(dec_seq, tm):
    n = HEAD_DIM // 4
    inv = ROPE_BASE ** (-jnp.arange(n, dtype=F32) / n)
    pos = jnp.arange(dec_seq)
    row = (pos // GRID_W).astype(F32)[:, None] * inv[None, :]
    col = (pos % GRID_W).astype(F32)[:, None] * inv[None, :]
    cos = jnp.concatenate([jnp.cos(row), jnp.cos(row), jnp.cos(col), jnp.cos(col)], axis=1)
    sin = jnp.concatenate([-jnp.sin(row), jnp.sin(row), -jnp.sin(col), jnp.sin(col)], axis=1)
    cos = jnp.concatenate([jnp.ones((tm, HEAD_DIM), F32), cos], axis=0)
    sin = jnp.concatenate([jnp.zeros((tm, HEAD_DIM), F32), sin], axis=0)
    return jnp.tile(cos, (1, LANES // HEAD_DIM)), jnp.tile(sin, (1, LANES // HEAD_DIM))


def _dup_halves(x):
    lane = _lane_iota(x.shape)
    r = pltpu.roll(x, HEAD_DIM, 1)
    lo = lane < HEAD_DIM
    return jnp.where(lo, x, r), jnp.where(lo, r, x)


def _stack_heads(qg):
    parts = []
    for u in range(2):
        qp = qg[:, u * LANES:(u + 1) * LANES]
        lo = _lane_iota(qp.shape) < HEAD_DIM
        zero = jnp.zeros_like(qp)
        parts.append(jnp.where(lo, qp, zero))
        parts.append(jnp.where(lo, zero, qp))
    return jnp.concatenate(parts, axis=0)


def _unstack_heads(o, rows):
    lo = _lane_iota((rows, LANES)) < HEAD_DIM
    pairs = [jnp.where(lo, o[(2 * u) * rows:(2 * u + 1) * rows], o[(2 * u + 1) * rows:(2 * u + 2) * rows])
             for u in range(2)]
    return jnp.concatenate(pairs, axis=1)


def _sink_col(sink_ref, j, rows):
    r = _row_iota((GQA_GROUP * rows, 1))
    col = jnp.full((GQA_GROUP * rows, 1), sink_ref[j * GQA_GROUP + GQA_GROUP - 1], F32)
    for g in range(GQA_GROUP - 2, -1, -1):
        col = jnp.where(r < (g + 1) * rows, sink_ref[j * GQA_GROUP + g], col)
    return col


def _ctx_attn_kernel(sink_ref, q_ref, k_ref, v_ref, o_ref, *, seq, nseq):
    for b in range(nseq):
        rs = slice(b * seq, (b + 1) * seq)
        kks = _dup_halves(k_ref[rs, :])
        vvs = _dup_halves(v_ref[rs, :])
        for j in range(N_KV_ATTN):
            qs = _stack_heads(q_ref[rs, j * 2 * LANES:(j + 1) * 2 * LANES])
            s = _dot_nt(qs, kks[j].astype(BF16))
            sink = _sink_col(sink_ref, j, seq)
            m = jnp.maximum(jnp.max(s, axis=-1, keepdims=True), sink)
            p = jnp.exp(s - m)
            den = jnp.sum(p, axis=-1, keepdims=True) + jnp.exp(sink - m)
            o = _dot(p.astype(BF16), vvs[j].astype(BF16)) / den
            o_ref[rs, j * 2 * LANES:(j + 1) * 2 * LANES] = _unstack_heads(o, seq).astype(BF16)


def _ctx_attention(sink, q, k, v, *, n_prompt, seq, nseq):
    t = q.shape[0]
    rows = seq * nseq
    blk = lambda w: pl.BlockSpec((rows, w), lambda i: (i, 0))
    return pl.pallas_call(
        functools.partial(_ctx_attn_kernel, seq=seq, nseq=nseq),
        grid=(n_prompt // rows,),
        in_specs=[pl.BlockSpec(memory_space=pltpu.SMEM), blk(Q_W), blk(KV_W), blk(KV_W)],
        out_specs=blk(Q_W),
        out_shape=jax.ShapeDtypeStruct((t, Q_W), BF16),
        compiler_params=_cparams(("arbitrary",)),
        name="ctx_attention",
    )(sink, q, k, v)


def _lat_attn_kernel(sink_ref, o_in_ref, q_ref, k_ref, v_ref, ck_ref, cv_ref, o_ref, *, qb, seq):
    del o_in_ref
    n = pl.program_id(1)
    nk = qb + 2 * WINDOW
    q0 = n * qb
    k0 = pl.multiple_of(jnp.clip(q0 - WINDOW, 0, seq - nk), WINDOW)
    kks = _dup_halves(k_ref[pl.ds(k0, nk), :])
    vvs = _dup_halves(v_ref[pl.ds(k0, nk), :])
    ckks = _dup_halves(ck_ref[0])
    cvvs = _dup_halves(cv_ref[0])
    rows = GQA_GROUP * qb
    qpos = q0 + (_row_iota((rows, nk)) & (qb - 1))
    kpos = k0 + _lane_iota((rows, nk))
    valid = jnp.abs(kpos - qpos) <= WINDOW
    for j in range(N_KV_ATTN):
        qs = _stack_heads(q_ref[:, j * 2 * LANES:(j + 1) * 2 * LANES])
        s_loc = jnp.where(valid, _dot_nt(qs, kks[j].astype(BF16)), NEG_INF)
        s_ctx = _dot_nt(qs, ckks[j].astype(BF16))
        sink = _sink_col(sink_ref, j, qb)
        m = jnp.maximum(jnp.maximum(jnp.max(s_loc, axis=-1, keepdims=True),
                                    jnp.max(s_ctx, axis=-1, keepdims=True)), sink)
        p_loc = jnp.exp(s_loc - m)
        p_ctx = jnp.exp(s_ctx - m)
        den = (jnp.sum(p_loc, axis=-1, keepdims=True) + jnp.sum(p_ctx, axis=-1, keepdims=True)
               + jnp.exp(sink - m))
        o = (_dot(p_loc.astype(BF16), vvs[j].astype(BF16)) + _dot(p_ctx.astype(BF16), cvvs[j].astype(BF16))) / den
        o_ref[:, j * 2 * LANES:(j + 1) * 2 * LANES] = _unstack_heads(o, qb).astype(BF16)


def _lat_attention(sink, o_prev, q, k, v, ck, cv, *, n_prompt, seq, qb):
    t = q.shape[0]
    nb = (t - n_prompt) // seq
    nq = seq // qb
    p_q = n_prompt // qb
    p_s = n_prompt // seq
    lc = ck.shape[1]
    qblk = pl.BlockSpec((qb, Q_W), lambda b, n: (p_q + b * nq + n, 0))
    kvblk = pl.BlockSpec((seq, KV_W), lambda b, n: (p_s + b, 0))
    cblk = pl.BlockSpec((1, lc, KV_W), lambda b, n: (b, 0, 0))
    return pl.pallas_call(
        functools.partial(_lat_attn_kernel, qb=qb, seq=seq),
        grid=(nb, nq),
        in_specs=[pl.BlockSpec(memory_space=pltpu.SMEM), pl.BlockSpec(memory_space=pl.ANY),
                  qblk, kvblk, kvblk, cblk, cblk],
        out_specs=qblk,
        out_shape=jax.ShapeDtypeStruct((t, Q_W), BF16),
        input_output_aliases={1: 0},
        compiler_params=_cparams(("arbitrary", "arbitrary")),
        name="lat_attention",
    )(sink, o_prev, q, k, v, ck, cv)


def _split3(x):
    hi = x.astype(BF16)
    r = x - hi.astype(F32)
    mid = r.astype(BF16)
    lo = (r - mid.astype(F32)).astype(BF16)
    return hi, mid, lo


def _gla_chunk(q, k, la, v0, v1, st, tri, causal, last_row):
    hi, mid, lo = _split3(la)
    b = _dot(tri, hi) + _dot(tri, mid) + _dot(tri, lo)
    btot = b[last_row:last_row + 1, :]
    qe = q * jnp.exp(b)
    ke = (k * jnp.exp(-b)).astype(BF16)
    kend = (k * jnp.exp(btot - b)).astype(BF16)
    dec = jnp.exp(btot)
    lo_lanes = _lane_iota(qe.shape) < GLA_DK
    zero = jnp.zeros_like(qe)
    stb = st.astype(BF16)
    outs = []
    for qm, vh in ((jnp.where(lo_lanes, qe, zero), v0), (jnp.where(lo_lanes, zero, qe), v1)):
        qm = qm.astype(BF16)
        a = jnp.where(causal, _dot_nt(qm, ke), 0.0).astype(BF16)
        outs.append(_dot(a, vh) + _dot_nt(qm, stb))
    kv = jnp.where(_lane_iota(st.shape) < GLA_DK, _dot_tn(v0, kend), _dot_tn(v1, kend))
    return outs[0], outs[1], st * dec + kv


def _gla_kernel(gq_ref, gk_ref, laf_ref, lab_ref, gv_ref, gr_ref, s0_ref, g_ref, o_in_ref,
                o_ref, sfin_ref, accf_ref, accb_ref, st_ref, *, seq, nseq):
    del o_in_ref
    c = GLA_CHUNK
    nc = seq // c
    ri = _row_iota((c, c))
    ci = _lane_iota((c, c))
    fwd_mask = ci <= ri
    bwd_mask = ci >= ri
    tri_f = jnp.where(fwd_mask, 1.0, 0.0).astype(BF16)
    tri_b = jnp.where(bwd_mask, 1.0, 0.0).astype(BF16)

    def step(base, n):
        rf = pl.multiple_of(base + n * c, c)
        rb = pl.multiple_of(base + (nc - 1 - n) * c, c)
        for d, (r0, la_ref, acc_ref, tri, mask, last) in enumerate(
                ((rf, laf_ref, accf_ref, tri_f, fwd_mask, c - 1), (rb, lab_ref, accb_ref, tri_b, bwd_mask, 0))):
            rows = pl.ds(r0, c)
            o0, o1, st = _gla_chunk(gq_ref[rows, :].astype(F32), gk_ref[rows, :].astype(F32), la_ref[rows, :],
                                    gv_ref[rows, 0:GLA_DV], gv_ref[rows, GLA_DV:2 * GLA_DV],
                                    st_ref[d], tri, mask, last)
            acc_ref[rows, 0:GLA_DV] = o0
            acc_ref[rows, GLA_DV:2 * GLA_DV] = o1
            st_ref[d] = st

    for b in range(nseq):
        base = b * seq
        st_ref[0] = s0_ref[b, 0, 0]
        st_ref[1] = s0_ref[b, 1, 0]
        if nc <= 4:
            for n in range(nc):
                step(base, n)
        else:
            lax.fori_loop(0, nc, lambda n, carry: (step(base, n), carry)[1], 0, unroll=2)
        sfin_ref[b, 0, 0] = st_ref[0]
        sfin_ref[b, 1, 0] = st_ref[1]

    g = g_ref[0]
    eb = min(256, seq)

    def epilogue(i, carry):
        rows = pl.ds(pl.multiple_of(i * eb, eb), eb)
        o = accf_ref[rows, :] + accb_ref[rows, :]
        outs = []
        for s in range(2):
            oh = o[:, s * GLA_DV:(s + 1) * GLA_DV]
            y = oh * lax.rsqrt(jnp.mean(oh * oh, axis=-1, keepdims=True) + LN_EPS) * g[s:s + 1, :]
            outs.append(y * _silu(gr_ref[rows, s * GLA_DV:(s + 1) * GLA_DV].astype(F32)))
        o_ref[rows, :] = jnp.concatenate(outs, axis=1).astype(BF16)
        return carry

    lax.fori_loop(0, nseq * seq // eb, epilogue, 0)


def _gla(gq, gk, la, gv, gr, s0, g, o_prev, *, row0, seq, nseq, nsteps):
    t = gq.shape[0]
    rows = seq * nseq
    r0 = row0 // rows
    npair = N_HEADS_GLA // 2
    blk = lambda w, off=0: pl.BlockSpec((rows, w), lambda i, p: (r0 + i, p + off))
    sblk = pl.BlockSpec((nseq, 2, 1, GLA_DV, LANES), lambda i, p: (i, 0, p, 0, 0))
    return pl.pallas_call(
        functools.partial(_gla_kernel, seq=seq, nseq=nseq),
        grid=(nsteps, npair),
        in_specs=[blk(LANES), blk(LANES), blk(LANES), blk(LANES, npair), blk(2 * GLA_DV), blk(2 * GLA_DV), sblk,
                  pl.BlockSpec((1, 2, GLA_DV), lambda i, p: (p, 0, 0)),
                  pl.BlockSpec(memory_space=pl.ANY)],
        out_specs=[blk(2 * GLA_DV), sblk],
        out_shape=[jax.ShapeDtypeStruct((t, GV_W), BF16),
                   jax.ShapeDtypeStruct((nsteps * nseq, 2, npair, GLA_DV, LANES), F32)],
        scratch_shapes=[pltpu.VMEM((rows, 2 * GLA_DV), F32), pltpu.VMEM((rows, 2 * GLA_DV), F32),
                        pltpu.VMEM((2, GLA_DV, LANES), F32)],
        input_output_aliases={8: 0},
        compiler_params=_cparams(("arbitrary", "arbitrary")),
        name="gla",
    )(gq, gk, la, la, gv, gr, s0, g.reshape(npair, 2, GLA_DV), o_prev)


def _state_to_pairs(s):
    b = s.shape[0]
    s = s.reshape(b, 2, N_HEADS_GLA // 2, 2, GLA_DK, GLA_DV)
    return s.transpose(0, 1, 2, 5, 3, 4).reshape(b, 2, N_HEADS_GLA // 2, GLA_DV, 2 * GLA_DK)


def _pairs_to_state(s):
    b = s.shape[0]
    s = s.reshape(b, 2, N_HEADS_GLA // 2, GLA_DV, 2, GLA_DK)
    return s.transpose(0, 1, 2, 4, 5, 3).reshape(b, 2, N_HEADS_GLA, GLA_DK, GLA_DV)


def _merge_kernel(x_ref, mod_ref, oa_ref, og_ref, sg_ref, wba_ref, wbg_ref, wo_ref, g1_ref, b1_ref,
                  wr_ref, rb_ref, x1_ref, h2_ref, gates_ref, *, alpha):
    d = x_ref.shape[1]
    m = mod_ref[0]
    merged = (sg_ref[:, 0:d].astype(F32) * _dot(oa_ref[...], wba_ref[...])
              + sg_ref[:, d:2 * d].astype(F32) * _dot(og_ref[...], wbg_ref[...]))
    y = _dot(merged.astype(BF16), wo_ref[...])
    x1 = _layer_norm(x_ref[...] * alpha + m[2:3, :] * y, g1_ref[...], b1_ref[...])
    x1_ref[...] = x1
    h2 = x1 * (1.0 + m[4:5, :]) + m[3:4, :]
    h2_ref[...] = h2.astype(BF16)
    s = _sigmoid(jnp.dot(h2, wr_ref[...], preferred_element_type=F32, precision=lax.Precision.HIGHEST))
    a = s + rb_ref[...]
    lane = _lane_iota(a.shape)
    gates = jnp.zeros_like(s)
    for _ in range(TOP_K):
        top = jnp.max(a, axis=-1, keepdims=True)
        idx = jnp.min(jnp.where(a == top, lane, LANES), axis=-1, keepdims=True)
        sel = lane == idx
        gates = jnp.where(sel, s, gates)
        a = jnp.where(sel, NEG_INF, a)
    gates_ref[...] = gates / jnp.sum(gates, axis=-1, keepdims=True) * ROUTE_SCALE


def _merge(x, mod_l, oa, og, sg, wba, wbg, wo, g1, b1, wr, rb, *, alpha, tm, n_prompt, dec_seq):
    t, d = x.shape
    npt = n_prompt // tm
    spt = dec_seq // tm

    def mod_row(i):
        return jnp.where(i < npt, 0, 1 + (i - npt) // spt)

    row = lambda w: pl.BlockSpec((tm, w), lambda i: (i, 0))
    full = lambda a: pl.BlockSpec(a.shape, lambda i: (0,) * a.ndim)
    return pl.pallas_call(
        functools.partial(_merge_kernel, alpha=alpha),
        grid=(t // tm,),
        in_specs=[row(d), pl.BlockSpec((1, 6, d), lambda i: (mod_row(i), 0, 0)),
                  row(Q_W), row(GV_W), row(2 * d), full(wba), full(wbg), full(wo), full(g1), full(b1),
                  full(wr), full(rb)],
        out_specs=[row(d), row(d), row(LANES)],
        out_shape=[jax.ShapeDtypeStruct((t, d), F32), jax.ShapeDtypeStruct((t, d), BF16),
                   jax.ShapeDtypeStruct((t, LANES), F32)],
        compiler_params=_cparams(("arbitrary",)),
        name="merge",
    )(x, mod_l, oa, og, sg, wba, wbg, wo, g1, b1, wr, rb)


def _moe_kernel(h_ref, x1_ref, mod_ref, gates_ref, w1_ref, w3_ref, w2_ref, ws1_ref, ws3_ref, ws2_ref,
                g2_ref, b2_ref, o_ref, acc_ref, *, alpha):
    e = pl.program_id(1)
    h = h_ref[...]

    @pl.when(e == 0)
    def _():
        hid = (_silu(_dot(h, ws1_ref[...])) * _dot(h, ws3_ref[...])).astype(BF16)
        acc_ref[...] = _dot(hid, ws2_ref[...])

    gates = gates_ref[...]
    gcol = jnp.sum(jnp.where(_lane_iota(gates.shape) == e, gates, 0.0), axis=-1, keepdims=True)
    hid = (_silu(_dot(h, w1_ref[0])) * _dot(h, w3_ref[0]) * gcol).astype(BF16)
    acc_ref[...] += _dot(hid, w2_ref[0])

    @pl.when(e == pl.num_programs(1) - 1)
    def _():
        m = mod_ref[0]
        o_ref[...] = _layer_norm(x1_ref[...] * alpha + m[5:6, :] * acc_ref[...], g2_ref[...], b2_ref[...])


def _moe(h2, x1, mod_l, gates, w1, w3, w2, ws1, ws3, ws2, g2, b2, *, alpha, tm, n_prompt, dec_seq):
    t, d = x1.shape
    ne = w1.shape[0]
    npt = n_prompt // tm
    spt = dec_seq // tm

    def mod_row(i):
        return jnp.where(i < npt, 0, 1 + (i - npt) // spt)

    row = lambda w: pl.BlockSpec((tm, w), lambda i, e: (i, 0))
    full = lambda a: pl.BlockSpec(a.shape, lambda i, e: (0,) * a.ndim)
    ex = lambda a: pl.BlockSpec((1,) + a.shape[1:], lambda i, e: (e, 0, 0))
    return pl.pallas_call(
        functools.partial(_moe_kernel, alpha=alpha),
        grid=(t // tm, ne),
        in_specs=[row(d), row(d), pl.BlockSpec((1, 6, d), lambda i, e: (mod_row(i), 0, 0)), row(LANES),
                  ex(w1), ex(w3), ex(w2), full(ws1), full(ws3), full(ws2), full(g2), full(b2)],
        out_specs=row(d),
        out_shape=jax.ShapeDtypeStruct((t, d), F32),
        scratch_shapes=[pltpu.VMEM((tm, d), F32)],
        compiler_params=_cparams(("arbitrary", "arbitrary")),
        name="moe",
    )(h2, x1, mod_l, gates, w1, w3, w2, ws1, ws3, ws2, g2, b2)


def kernel(x_prompt, x_sample, cache_k, cache_v, state_gla, c, c_ctx, w_mod, b_mod, w_in, b_gate, w_a2_f, b_a_f, w_a2_b, b_a_b, sink, gla_g, w_ba, w_bg, w_o, ln1_g, ln1_b, w_router, router_bias, w1, w3, w2, ws1, ws3, ws2, ln2_g, ln2_b):
    batch, seq, d = x_prompt.shape
    dec_batch, dec_seq, _ = x_sample.shape
    depth = w_mod.shape[0]
    ne = w_router.shape[-1]
    past = cache_k.shape[3]
    alpha = (2.0 * depth) ** 0.25
    n_prompt = batch * seq
    n_sample = dec_batch * dec_seq
    t = n_prompt + n_sample
    assert dec_batch + 1 <= MOD_ROWS and ne <= LANES

    tm_proj = min(512, seq * batch, dec_seq)
    tm_moe = min(1024, seq * batch, dec_seq)
    qb = min(256, dec_seq // 2)
    ctx_nseq = min(4, batch)
    gla_nseq = min(4, batch)

    cond = jnp.zeros((MOD_ROWS, d), F32).at[0].set(c_ctx).at[1:1 + dec_batch].set(c)
    mod = _modulation(cond, w_mod, b_mod).reshape(depth, MOD_ROWS, 6, d)
    cos_t, sin_t = _rope_tables(dec_seq, tm_proj)

    x = jnp.concatenate([x_prompt.reshape(n_prompt, d), x_sample.reshape(n_sample, d)], axis=0)
    zeros_state = jnp.zeros((batch, 2, N_HEADS_GLA // 2, GLA_DV, LANES), F32)
    new_k, new_v, new_s = [], [], []
    o_main = Q_W + 2 * KV_W + 2 * GK_W + 2 * GV_W
    for l in range(depth):
        wl = w_in[l]
        wm = wl[:, :o_main].astype(BF16)
        wab = jnp.pad(wl[:, o_main:o_main + 2 * GLA_RANK], ((0, 0), (0, LANES - 2 * GLA_RANK))).astype(BF16)
        wg = wl[:, o_main + 2 * GLA_RANK:].astype(BF16)
        wa2 = jnp.zeros((LANES, 2 * GK_W), F32)
        wa2 = wa2.at[0:GLA_RANK, 0:GK_W].set(w_a2_f[l]).at[GLA_RANK:2 * GLA_RANK, GK_W:].set(w_a2_b[l])
        ba = jnp.concatenate([b_a_f[l], b_a_b[l]])[None, :]
        bg = b_gate[l].reshape(1, 2 * d)

        q, k, v, gq, gk, gv, gr, la, sg = _in_proj(
            x, mod[l], wm, wab, wg, wa2, ba, bg, cos_t, sin_t, tm=tm_proj, n_prompt=n_prompt, dec_seq=dec_seq)

        new_k.append(k[:n_prompt].reshape(batch, seq, N_KV_ATTN, HEAD_DIM).transpose(0, 2, 1, 3))
        new_v.append(v[:n_prompt].reshape(batch, seq, N_KV_ATTN, HEAD_DIM).transpose(0, 2, 1, 3))

        ck = cache_k[:, l].transpose(0, 2, 1, 3).reshape(dec_batch, past, KV_W)
        cv = cache_v[:, l].transpose(0, 2, 1, 3).reshape(dec_batch, past, KV_W)
        oa = _ctx_attention(sink[l], q, k, v, n_prompt=n_prompt, seq=seq, nseq=ctx_nseq)
        oa = _lat_attention(sink[l], oa, q, k, v, ck, cv, n_prompt=n_prompt, seq=dec_seq, qb=qb)

        og0 = jnp.zeros((t, GV_W), BF16)
        og, s_ctx = _gla(gq, gk, la, gv, gr, zeros_state, gla_g[l], og0,
                         row0=0, seq=seq, nseq=gla_nseq, nsteps=batch // gla_nseq)
        og, _ = _gla(gq, gk, la, gv, gr, _state_to_pairs(state_gla[:, l]), gla_g[l], og,
                     row0=n_prompt, seq=dec_seq, nseq=1, nsteps=dec_batch)
        new_s.append(_pairs_to_state(s_ctx))

        wr = jnp.pad(w_router[l], ((0, 0), (0, LANES - ne)))
        rb = jnp.pad(router_bias[l], (0, LANES - ne), constant_values=NEG_INF)[None, :]
        x1, h2, gates = _merge(x, mod[l], oa, og, sg, w_ba[l].astype(BF16), w_bg[l].astype(BF16),
                               w_o[l].astype(BF16), ln1_g[l][None, :], ln1_b[l][None, :], wr, rb,
                               alpha=alpha, tm=tm_proj, n_prompt=n_prompt, dec_seq=dec_seq)
        x = _moe(h2, x1, mod[l], gates, w1[l].astype(BF16), w3[l].astype(BF16), w2[l].astype(BF16),
                 ws1[l].astype(BF16), ws3[l].astype(BF16), ws2[l].astype(BF16),
                 ln2_g[l][None, :], ln2_b[l][None, :], alpha=alpha, tm=tm_moe, n_prompt=n_prompt, dec_seq=dec_seq)

    return (x[:n_prompt].reshape(batch, seq, d), x[n_prompt:].reshape(dec_batch, dec_seq, d),
            jnp.stack(new_k, axis=1), jnp.stack(new_v, axis=1), jnp.stack(new_s, axis=1))
```
